```python
import jax, jax.numpy as jnp
from jax import lax
import numpy as np

D_MODEL = 1024
BATCH = 2
SEQ = 16384
DEPTH = 4

CHUNK = 64
SGU_BLOCK = 128
D_A = 1024
A_GROUPS = 8
A_GROUP_DIM = D_A // A_GROUPS
D_B = 1024
POOL_WINDOWS = (2, 4, 8, 16)
B_GROUPS = len(POOL_WINDOWS)
B_GROUP_DIM = D_B // B_GROUPS
D_C = 1024
CONV_WIDTH = 3
N_BRANCH = 3
SPLIT_SIZES = (D_A, D_A, D_A, D_B, D_B, D_C, D_C, D_C, D_C, N_BRANCH * D_MODEL)
N_IN = sum(SPLIT_SIZES)
SPLIT_OFFSETS = tuple(int(o) for o in np.cumsum(SPLIT_SIZES)[:-1])
RMS_EPS = 1e-6
LN_EPS = 1e-5

kernel_name = "hybrid_gated_parallel_mixers"


def rmsnorm(x, g):
    xf = x.astype(jnp.float32)
    y = xf * lax.rsqrt(jnp.mean(xf * xf, axis=-1, keepdims=True) + RMS_EPS)
    return (y * g.astype(jnp.float32)).astype(x.dtype)


def layernorm(x, g, b):
    xf = x.astype(jnp.float32)
    mu = jnp.mean(xf, axis=-1, keepdims=True)
    var = jnp.mean(jnp.square(xf - mu), axis=-1, keepdims=True)
    y = (xf - mu) * lax.rsqrt(var + LN_EPS)
    return (y * g.astype(jnp.float32) + b.astype(jnp.float32)).astype(x.dtype)


def chunk_causal_mask():
    c = jnp.arange(SGU_BLOCK) // CHUNK
    return c[None, :] <= c[:, None]


def spatial_gating(u, v, w_s, b_s, ln_g, ln_b):
    bsz, s, _ = v.shape
    v = layernorm(v, ln_g, ln_b)
    vb = v.reshape(bsz, s // SGU_BLOCK, SGU_BLOCK, A_GROUPS, A_GROUP_DIM)
    w = jnp.where(chunk_causal_mask()[None], w_s, jnp.zeros_like(w_s))
    mixed = jnp.einsum('gij,bnjgc->bnigc', w, vb) + b_s.T[:, :, None]
    return u * mixed.reshape(bsz, s, D_A)


def multiscale_pool(p, w_g, b_g, scale):
    bsz, s, _ = p.shape
    pf = p.astype(jnp.float32)
    csum = jnp.cumsum(pf, axis=1)
    pos1 = jnp.arange(1, s + 1, dtype=jnp.int32)
    outs = []
    for k, win in enumerate(POOL_WINDOWS):
        sl = slice(k * B_GROUP_DIM, (k + 1) * B_GROUP_DIM)
        cg = csum[..., sl]
        shifted = jnp.pad(cg, ((0, 0), (win, 0), (0, 0)))[:, :s]
        count = jnp.minimum(pos1, win).astype(jnp.float32)[:, None]
        outs.append((cg - shifted) / count - pf[..., sl])
    d = jnp.concatenate(outs, axis=-1).astype(p.dtype)
    d = d.reshape(bsz, s, B_GROUPS, B_GROUP_DIM)
    y = jnp.einsum('bsgc,gcd->bsgd', d, w_g).reshape(bsz, s, D_B) + b_g
    return y * scale


def causal_depthwise_conv(h, w, b):
    s = h.shape[1]
    hp = jnp.pad(h, ((0, 0), (CONV_WIDTH - 1, 0), (0, 0)))
    y = sum(w[k] * hp[:, k:k + s] for k in range(CONV_WIDTH))
    return y + b


def setup_inputs(seed: int = 0) -> dict:
    key = jax.random.key(seed)
    ks = jax.random.split(key, 20)

    def nrm(k, shape, scale):
        return jax.random.normal(k, shape, jnp.float32) * scale

    L = DEPTH
    return {
        "x": nrm(ks[0], (BATCH, SEQ, D_MODEL), 1.0),
        "norm_g": 1.0 + nrm(ks[1], (L, D_MODEL), 0.05),
        "w_in": nrm(ks[2], (L, D_MODEL, N_IN), D_MODEL ** -0.5),
        "sgu_ln_g": 1.0 + nrm(ks[3], (L, D_A), 0.05),
        "sgu_ln_b": nrm(ks[4], (L, D_A), 0.02),
        "sgu_w": nrm(ks[5], (L, A_GROUPS, SGU_BLOCK, SGU_BLOCK), SGU_BLOCK ** -0.5),
        "sgu_b": 1.0 + nrm(ks[6], (L, A_GROUPS, SGU_BLOCK), 0.05),
        "pool_w": nrm(ks[7], (L, B_GROUPS, B_GROUP_DIM, B_GROUP_DIM), B_GROUP_DIM ** -0.5),
        "pool_b": nrm(ks[8], (L, D_B), 0.01),
        "pool_scale": 1.0 + nrm(ks[9], (L, D_B), 0.1),
        "conv_w": nrm(ks[10], (L, CONV_WIDTH, D_C), CONV_WIDTH ** -0.5),
        "conv_b": nrm(ks[11], (L, D_C), 0.01),
        "w_branch_a": nrm(ks[12], (L, D_A, D_MODEL), D_A ** -0.5),
        "w_branch_b": nrm(ks[13], (L, D_B, D_MODEL), D_B ** -0.5),
        "w_branch_c": nrm(ks[14], (L, D_C, D_MODEL), D_C ** -0.5),
        "w_out": nrm(ks[15], (L, D_MODEL, D_MODEL), D_MODEL ** -0.5),
        "final_g": 1.0 + nrm(ks[16], (D_MODEL,), 0.05),
    }


def reference(x, norm_g, w_in, sgu_ln_g, sgu_ln_b, sgu_w, sgu_b, pool_w, pool_b,
              pool_scale, conv_w, conv_b, w_branch_a, w_branch_b, w_branch_c,
              w_out, final_g):
    bsz, s, _ = x.shape
    for l in range(DEPTH):
        h = rmsnorm(x, norm_g[l])
        proj = jnp.einsum('bsd,dn->bsn', h, w_in[l])
        a_u, a_v, a_z, b_p, b_z, c_h, c_b, c_c, c_z, gate_logits = jnp.split(
            proj, SPLIT_OFFSETS, axis=-1)

        ya = spatial_gating(jax.nn.gelu(a_u), jax.nn.gelu(a_v), sgu_w[l], sgu_b[l],
                            sgu_ln_g[l], sgu_ln_b[l]) * jax.nn.silu(a_z)
        yb = multiscale_pool(b_p, pool_w[l], pool_b[l], pool_scale[l]) * jax.nn.silu(b_z)
        yc = c_b * causal_depthwise_conv(c_c * c_h, conv_w[l], conv_b[l]) * jax.nn.silu(c_z)

        gates = jax.nn.sigmoid(gate_logits.reshape(bsz, s, N_BRANCH, D_MODEL))
        merged = (gates[:, :, 0] * jnp.einsum('bsc,cd->bsd', ya, w_branch_a[l])
                  + gates[:, :, 1] * jnp.einsum('bsc,cd->bsd', yb, w_branch_b[l])
                  + gates[:, :, 2] * jnp.einsum('bsc,cd->bsd', yc, w_branch_c[l]))
        x = x + jnp.einsum('bsd,de->bse', merged, w_out[l])
    return rmsnorm(x, final_g)
```

```python
import functools
import math

import jax
import jax.numpy as jnp
from jax import lax
from jax.experimental import pallas as pl
from jax.experimental.pallas import tpu as pltpu

D_MODEL = 1024
CHUNK = 64
SGU_BLOCK = 128
A_GROUPS = 8
A_GROUP_DIM = 128
POOL_WINDOWS = (2, 4, 8, 16)
B_GROUP_DIM = 256
CONV_WIDTH = 3
N_BRANCH = 3
N_IN = 12 * D_MODEL
RMS_EPS = 1e-6
LN_EPS = 1e-5

POOL_HALO = 16
CONV_HALO = 8
V7X_VMEM_BYTES = 64 * 1024 * 1024

_OFF = {name: i * D_MODEL for i, name in enumerate(
    ("a_u", "a_v", "a_z", "b_p", "b_z", "c_h", "c_b", "c_c", "c_z", "g0", "g1", "g2"))}


def _sigmoid(z):
    return 0.5 * jnp.tanh(0.5 * z) + 0.5


def _silu(z):
    return z * _sigmoid(z)


def _gelu_tanh(x):
    c = math.sqrt(2.0 / math.pi)
    return x * (0.5 * (1.0 + jnp.tanh(c * (x + 0.044715 * (x * x * x)))))


def _dot(a, b):
    return jnp.dot(a, b, preferred_element_type=jnp.float32)


def _layer_kernel(x_ref, norm_g_ref, w_in_ref, ln_g_ref, ln_b_ref, sgu_w_ref, sgu_bias_ref,
                  pool_w_ref, pool_b_ref, pool_scale_ref, conv_w_ref, conv_b_ref,
                  wa_ref, wb_ref, wc_ref, wo_ref, final_g_ref,
                  o_ref, carry_p_ref, carry_t_ref, *, tm, tiles_per_seq, apply_final):
    step = pl.program_id(0)
    tile_in_seq = lax.rem(step, tiles_per_seq)

    @pl.when(tile_in_seq == 0)
    def _():
        carry_p_ref[...] = jnp.zeros_like(carry_p_ref)
        carry_t_ref[...] = jnp.zeros_like(carry_t_ref)

    x = x_ref[...]
    ms = jnp.mean(x * x, axis=-1, keepdims=True)
    h = (x * lax.rsqrt(ms + RMS_EPS) * norm_g_ref[...]).astype(jnp.bfloat16)

    def proj(name):
        off = _OFF[name]
        return _dot(h, w_in_ref[:, off:off + D_MODEL])

    u = _gelu_tanh(proj("a_u"))
    v = _gelu_tanh(proj("a_v"))
    mu = jnp.mean(v, axis=-1, keepdims=True)
    vc = v - mu
    var = jnp.mean(vc * vc, axis=-1, keepdims=True)
    vb = (vc * lax.rsqrt(var + LN_EPS) * ln_g_ref[...] + ln_b_ref[...]).astype(jnp.bfloat16)

    nblk = tm // SGU_BLOCK
    ri = lax.broadcasted_iota(jnp.int32, (SGU_BLOCK, SGU_BLOCK), 0)
    ci = lax.broadcasted_iota(jnp.int32, (SGU_BLOCK, SGU_BLOCK), 1)
    allowed = (ci // CHUNK) <= (ri // CHUNK)
    mixed_g = []
    for g in range(A_GROUPS):
        w_g = jnp.where(allowed, sgu_w_ref[g], 0.0).astype(jnp.bfloat16)
        cols = slice(g * A_GROUP_DIM, (g + 1) * A_GROUP_DIM)
        rhs = jnp.concatenate(
            [vb[r * SGU_BLOCK:(r + 1) * SGU_BLOCK, cols] for r in range(nblk)], axis=1)
        mixed_g.append(_dot(w_g, rhs))
    mixed = jnp.concatenate(
        [jnp.concatenate([mixed_g[g][:, r * SGU_BLOCK:(r + 1) * SGU_BLOCK]
                          for g in range(A_GROUPS)], axis=1)
         for r in range(nblk)], axis=0)
    bias_a = jnp.concatenate([sgu_bias_ref[...]] * nblk, axis=0)
    ya = u * (mixed + bias_a) * _silu(proj("a_z"))
    merged = _sigmoid(proj("g0")) * _dot(ya.astype(jnp.bfloat16), wa_ref[...])

    p = proj("b_p")
    pe = jnp.concatenate([carry_p_ref[...], p], axis=0)
    carry_p_ref[...] = p[tm - POOL_HALO:, :]
    pos1 = lax.broadcasted_iota(jnp.int32, (tm, B_GROUP_DIM), 0) + (tile_in_seq * tm + 1)
    d_groups = []
    for k, win in enumerate(POOL_WINDOWS):
        cols = slice(k * B_GROUP_DIM, (k + 1) * B_GROUP_DIM)
        s = pe[:, cols]
        span = 1
        while span < win:
            s = s + pltpu.roll(s, span, axis=0)
            span *= 2
        count = jnp.minimum(pos1, win).astype(jnp.float32)
        d_groups.append(s[POOL_HALO:, :] / count - p[:, cols])
    yb_groups = []
    for k in range(len(POOL_WINDOWS)):
        yb_groups.append(_dot(d_groups[k].astype(jnp.bfloat16), pool_w_ref[k]))
    yb = jnp.concatenate(yb_groups, axis=1)
    yb = (yb + pool_b_ref[...]) * pool_scale_ref[...] * _silu(proj("b_z"))
    merged = merged + _sigmoid(proj("g1")) * _dot(yb.astype(jnp.bfloat16), wb_ref[...])

    t = proj("c_c") * proj("c_h")
    te = jnp.concatenate([carry_t_ref[...], t], axis=0)
    carry_t_ref[...] = t[tm - CONV_HALO:, :]
    t1 = pltpu.roll(te, 1, axis=0)[CONV_HALO:, :]
    t2 = pltpu.roll(te, 2, axis=0)[CONV_HALO:, :]
    conv = (conv_w_ref[0:1, :] * t2 + conv_w_ref[1:2, :] * t1 + conv_w_ref[2:3, :] * t
            + conv_b_ref[...])
    yc = proj("c_b") * conv * _silu(proj("c_z"))
    merged = merged + _sigmoid(proj("g2")) * _dot(yc.astype(jnp.bfloat16), wc_ref[...])

    out = x + _dot(merged.astype(jnp.bfloat16), wo_ref[...])
    if apply_final:
        ms_o = jnp.mean(out * out, axis=-1, keepdims=True)
        out = out * lax.rsqrt(ms_o + RMS_EPS) * final_g_ref[...]
    o_ref[...] = out


def _resident(shape):
    nd = len(shape)
    return pl.BlockSpec(shape, lambda i: (0,) * nd, pipeline_mode=pl.Buffered(1))


def _layer_call(x2d, params, *, tm, tiles_per_seq, apply_final):
    n_tok = x2d.shape[0]
    kern = functools.partial(_layer_kernel, tm=tm, tiles_per_seq=tiles_per_seq,
                             apply_final=apply_final)
    in_specs = [pl.BlockSpec((tm, D_MODEL), lambda i: (i, 0))]
    in_specs += [_resident(p.shape) for p in params]
    weight_bytes = sum(p.size * p.dtype.itemsize for p in params)
    tile_bytes = tm * D_MODEL * 4
    vmem_need = weight_bytes + 4 * tile_bytes + 24 * tile_bytes
    return pl.pallas_call(
        kern,
        grid=(n_tok // tm,),
        in_specs=in_specs,
        out_specs=pl.BlockSpec((tm, D_MODEL), lambda i: (i, 0)),
        out_shape=jax.ShapeDtypeStruct((n_tok, D_MODEL), jnp.float32),
        scratch_shapes=[pltpu.VMEM((POOL_HALO, D_MODEL), jnp.float32),
                        pltpu.VMEM((CONV_HALO, D_MODEL), jnp.float32)],
        compiler_params=pltpu.CompilerParams(
            dimension_semantics=("arbitrary",),
            vmem_limit_bytes=min(vmem_need, V7X_VMEM_BYTES - 2 * 1024 * 1024)),
        name="mixer_layer_final" if apply_final else "mixer_layer",
    )(x2d, *params)


def kernel(x, norm_g, w_in, sgu_ln_g, sgu_ln_b, sgu_w, sgu_b, pool_w, pool_b, pool_scale,
           conv_w, conv_b, w_branch_a, w_branch_b, w_branch_c, w_out, final_g):
    bsz, seq, d = x.shape
    depth = w_in.shape[0]
    assert d == D_MODEL and w_in.shape[2] == N_IN
    tm = 256
    assert seq % tm == 0 and tm % SGU_BLOCK == 0
    bf = jnp.bfloat16
    row = lambda a: a.reshape(1, -1)
    x2d = x.reshape(bsz * seq, d)
    for l in range(depth):
        sgu_bias = jnp.repeat(sgu_b[l].T, A_GROUP_DIM, axis=1)
        params = (row(norm_g[l]), w_in[l].astype(bf), row(sgu_ln_g[l]), row(sgu_ln_b[l]),
                  sgu_w[l], sgu_bias, pool_w[l].astype(bf), row(pool_b[l]),
                  row(pool_scale[l]), conv_w[l], row(conv_b[l]),
                  w_branch_a[l].astype(bf), w_branch_b[l].astype(bf),
                  w_branch_c[l].astype(bf), w_out[l].astype(bf), row(final_g))
        x2d = _layer_call(x2d, params, tm=tm, tiles_per_seq=seq // tm,
                          apply_final=(l == depth - 1))
    return x2d.reshape(bsz, seq, d)
```

```python
import functools
import math

import jax
import jax.numpy as jnp
from jax import lax
from jax.experimental import pallas as pl
from jax.experimental.pallas import tpu as pltpu

D_MODEL = 1024
CHUNK = 64
SGU_BLOCK = 128
A_GROUPS = 8
A_GROUP_DIM = 128
POOL_WINDOWS = (2, 4, 8, 16)
B_GROUP_DIM = 256
CONV_WIDTH = 3
N_BRANCH = 3
N_IN = 12 * D_MODEL
RMS_EPS = 1e-6
LN_EPS = 1e-5

POOL_HALO = 16
CONV_HALO = 8
V7X_VMEM_BYTES = 64 * 1024 * 1024

_OFF = {name: i * D_MODEL for i, name in enumerate(
    ("a_u", "a_v", "a_z", "b_p", "b_z", "c_h", "c_b", "c_c", "c_z", "g0", "g1", "g2"))}


def _sigmoid(z):
    return 0.5 * jnp.tanh(0.5 * z) + 0.5


def _silu(z):
    return z * _sigmoid(z)


def _gelu_tanh(x):
    c = math.sqrt(2.0 / math.pi)
    return x * (0.5 * (1.0 + jnp.tanh(c * (x + 0.044715 * (x * x * x)))))


def _dot(a, b):
    return jnp.dot(a, b, preferred_element_type=jnp.float32)


def _layer_kernel(x_ref, norm_g_ref, w_in_ref, ln_g_ref, ln_b_ref, sgu_w_ref, sgu_bias_ref,
                  pool_w_ref, pool_b_ref, pool_scale_ref, conv_w_ref, conv_b_ref,
                  wa_ref, wb_ref, wc_ref, wo_ref, final_g_ref,
                  o_ref, carry_p_ref, carry_t_ref, *, tm, tiles_per_seq, apply_final):
    step = pl.program_id(0)
    tile_in_seq = lax.rem(step, tiles_per_seq)

    @pl.when(tile_in_seq == 0)
    def _():
        carry_p_ref[...] = jnp.zeros_like(carry_p_ref)
        carry_t_ref[...] = jnp.zeros_like(carry_t_ref)

    x = x_ref[...]
    ms = jnp.mean(x * x, axis=-1, keepdims=True)
    h = (x * lax.rsqrt(ms + RMS_EPS) * norm_g_ref[...]).astype(jnp.bfloat16)

    def proj(name):
        off = _OFF[name]
        return _dot(h, w_in_ref[:, off:off + D_MODEL])

    u = _gelu_tanh(proj("a_u"))
    v = _gelu_tanh(proj("a_v"))
    mu = jnp.mean(v, axis=-1, keepdims=True)
    vc = v - mu
    var = jnp.mean(vc * vc, axis=-1, keepdims=True)
    vb = (vc * lax.rsqrt(var + LN_EPS) * ln_g_ref[...] + ln_b_ref[...]).astype(jnp.bfloat16)

    nblk = tm // SGU_BLOCK
    ri = lax.broadcasted_iota(jnp.int32, (SGU_BLOCK, SGU_BLOCK), 0)
    ci = lax.broadcasted_iota(jnp.int32, (SGU_BLOCK, SGU_BLOCK), 1)
    allowed = (ci // CHUNK) <= (ri // CHUNK)
    mixed_g = []
    for g in range(A_GROUPS):
        w_g = jnp.where(allowed, sgu_w_ref[g], 0.0).astype(jnp.bfloat16)
        cols = slice(g * A_GROUP_DIM, (g + 1) * A_GROUP_DIM)
        rhs = jnp.concatenate(
            [vb[r * SGU_BLOCK:(r + 1) * SGU_BLOCK, cols] for r in range(nblk)], axis=1)
        mixed_g.append(_dot(w_g, rhs))
    mixed = jnp.concatenate(
        [jnp.concatenate([mixed_g[g][:, r * SGU_BLOCK:(r + 1) * SGU_BLOCK]
                          for g in range(A_GROUPS)], axis=1)
         for r in range(nblk)], axis=0)
    bias_a = jnp.concatenate([sgu_bias_ref[...]] * nblk, axis=0)
    ya = u * (mixed + bias_a) * _silu(proj("a_z"))
    merged = _sigmoid(proj("g0")) * _dot(ya.astype(jnp.bfloat16), wa_ref[...])

    p = proj("b_p")
    pe = jnp.concatenate([carry_p_ref[...], p], axis=0)
    carry_p_ref[...] = p[tm - POOL_HALO:, :]
    pos1 = lax.broadcasted_iota(jnp.int32, (tm, B_GROUP_DIM), 0) + (tile_in_seq * tm + 1)
    d_groups = []
    for k, win in enumerate(POOL_WINDOWS):
        cols = slice(k * B_GROUP_DIM, (k + 1) * B_GROUP_DIM)
        s = pe[:, cols]
        span = 1
        while span < win:
            s = s + pltpu.roll(s, span, axis=0)
            span *= 2
        count = jnp.minimum(pos1, win).astype(jnp.float32)
        d_groups.append(s[POOL_HALO:, :] / count - p[:, cols])
    yb_groups = []
    for k in range(len(POOL_WINDOWS)):
        yb_groups.append(_dot(d_groups[k].astype(jnp.bfloat16), pool_w_ref[k]))
    yb = jnp.concatenate(yb_groups, axis=1)
    yb = (yb + pool_b_ref[...]) * pool_scale_ref[...] * _silu(proj("b_z"))
    merged = merged + _sigmoid(proj("g1")) * _dot(yb.astype(jnp.bfloat16), wb_ref[...])

    t = proj("c_c") * proj("c_h")
    te = jnp.concatenate([carry_t_ref[...], t], axis=0)
    carry_t_ref[...] = t[tm - CONV_HALO:, :]
    t1 = pltpu.roll(te, 1, axis=0)[CONV_HALO:, :]
    t2 = pltpu.roll(te, 2, axis=0)[CONV_HALO:, :]
    conv = (conv_w_ref[0:1, :] * t2 + conv_w_ref[1:2, :] * t1 + conv_w_ref[2:3, :] * t
            + conv_b_ref[...])
    yc = proj("c_b") * conv * _silu(proj("c_z"))
    merged = merged + _sigmoid(proj("g2")) * _dot(yc.astype(jnp.bfloat16), wc_ref[...])

    out = x + _dot(merged.astype(jnp.bfloat16), wo_ref[...])
    if apply_final:
        ms_o = jnp.mean(out * out, axis=-1, keepdims=True)
        out = out * lax.rsqrt(ms_o + RMS_EPS) * final_g_ref[...]
    o_ref[...] = out


def _resident(shape, layer):
    nd = len(shape) - 1
    return pl.BlockSpec((None,) + tuple(shape[1:]), lambda i: (layer,) + (0,) * nd,
                        pipeline_mode=pl.Buffered(1))


def _layer_call(x2d, params, layer, *, tm, tiles_per_seq, apply_final):
    n_tok = x2d.shape[0]
    kern = functools.partial(_layer_kernel, tm=tm, tiles_per_seq=tiles_per_seq,
                             apply_final=apply_final)
    in_specs = [pl.BlockSpec((tm, D_MODEL), lambda i: (i, 0))]
    in_specs += [_resident(p.shape, layer) for p in params]
    weight_bytes = sum(p.size // p.shape[0] * p.dtype.itemsize for p in params)
    tile_bytes = tm * D_MODEL * 4
    vmem_need = weight_bytes + 4 * tile_bytes + 24 * tile_bytes
    return pl.pallas_call(
        kern,
        grid=(n_tok // tm,),
        in_specs=in_specs,
        out_specs=pl.BlockSpec((tm, D_MODEL), lambda i: (i, 0)),
        out_shape=jax.ShapeDtypeStruct((n_tok, D_MODEL), jnp.float32),
        scratch_shapes=[pltpu.VMEM((POOL_HALO, D_MODEL), jnp.float32),
                        pltpu.VMEM((CONV_HALO, D_MODEL), jnp.float32)],
        compiler_params=pltpu.CompilerParams(
            dimension_semantics=("arbitrary",),
            vmem_limit_bytes=min(vmem_need, V7X_VMEM_BYTES - 2 * 1024 * 1024)),
        name="mixer_layer_final" if apply_final else "mixer_layer",
    )(x2d, *params)


def kernel(x, norm_g, w_in, sgu_ln_g, sgu_ln_b, sgu_w, sgu_b, pool_w, pool_b, pool_scale,
           conv_w, conv_b, w_branch_a, w_branch_b, w_branch_c, w_out, final_g):
    bsz, seq, d = x.shape
    depth = w_in.shape[0]
    assert d == D_MODEL and w_in.shape[2] == N_IN
    tm = 512
    assert seq % tm == 0 and tm % SGU_BLOCK == 0
    bf = jnp.bfloat16
    rows = lambda a: a.reshape(a.shape[0], 1, -1)
    sgu_bias = jnp.repeat(jnp.swapaxes(sgu_b, 1, 2), A_GROUP_DIM, axis=2)
    final_rows = jnp.broadcast_to(final_g.reshape(1, 1, -1), (depth, 1, d))
    params = (rows(norm_g), w_in.astype(bf), rows(sgu_ln_g), rows(sgu_ln_b), sgu_w, sgu_bias,
              pool_w.astype(bf), rows(pool_b), rows(pool_scale), conv_w, rows(conv_b),
              w_branch_a.astype(bf), w_branch_b.astype(bf), w_branch_c.astype(bf),
              w_out.astype(bf), final_rows)
    x2d = x.reshape(bsz * seq, d)
    for l in range(depth):
        x2d = _layer_call(x2d, params, l, tm=tm, tiles_per_seq=seq // tm,
                          apply_final=(l == depth - 1))
    return x2d.reshape(bsz, seq, d)
```

```python
import functools
import math

import jax
import jax.numpy as jnp
from jax import lax
from jax.experimental import pallas as pl
from jax.experimental.pallas import tpu as pltpu

D_MODEL = 1024
CHUNK = 64
SGU_BLOCK = 128
A_GROUPS = 8
A_GROUP_DIM = 128
POOL_WINDOWS = (2, 4, 8, 16)
B_GROUP_DIM = 256
CONV_WIDTH = 3
N_BRANCH = 3
N_IN = 12 * D_MODEL
RMS_EPS = 1e-6
LN_EPS = 1e-5

POOL_HALO = 16
CONV_HALO = 8
V7X_VMEM_BYTES = 64 * 1024 * 1024

_OFF = {name: i * D_MODEL for i, name in enumerate(
    ("a_u", "a_v", "a_z", "b_p", "b_z", "c_h", "c_b", "c_c", "c_z", "g0", "g1", "g2"))}


def _sigmoid(z):
    return 0.5 * jnp.tanh(0.5 * z) + 0.5


def _silu(z):
    return z * _sigmoid(z)


def _gelu_tanh(x):
    c = math.sqrt(2.0 / math.pi)
    return x * (0.5 * (1.0 + jnp.tanh(c * (x + 0.044715 * (x * x * x)))))


def _dot(a, b):
    return jnp.dot(a, b, preferred_element_type=jnp.float32)


def _wdot(a, w_words):
    return _dot(a, pltpu.bitcast(w_words, jnp.bfloat16))


def _pack_bf16_rows(w):
    wb = w.astype(jnp.bfloat16)
    lo = lax.bitcast_convert_type(wb[..., 0::2, :], jnp.uint16).astype(jnp.uint32)
    hi = lax.bitcast_convert_type(wb[..., 1::2, :], jnp.uint16).astype(jnp.uint32)
    return lo | (hi << 16)


def _layer_kernel(x_ref, norm_g_ref, w_in_ref, ln_g_ref, ln_b_ref, sgu_w_ref, sgu_bias_ref,
                  pool_w_ref, pool_b_ref, pool_scale_ref, conv_w_ref, conv_b_ref,
                  wa_ref, wb_ref, wc_ref, wo_ref, final_g_ref,
                  o_ref, carry_p_ref, carry_t_ref, *, tm, tiles_per_seq, apply_final):
    step = pl.program_id(0)
    tile_in_seq = lax.rem(step, tiles_per_seq)

    @pl.when(tile_in_seq == 0)
    def _():
        carry_p_ref[...] = jnp.zeros_like(carry_p_ref)
        carry_t_ref[...] = jnp.zeros_like(carry_t_ref)

    x = x_ref[...]
    ms = jnp.mean(x * x, axis=-1, keepdims=True)
    h = (x * lax.rsqrt(ms + RMS_EPS) * norm_g_ref[...]).astype(jnp.bfloat16)

    def proj(name):
        off = _OFF[name]
        return _wdot(h, w_in_ref[:, off:off + D_MODEL])

    nblk = tm // SGU_BLOCK

    v = _gelu_tanh(proj("a_v"))
    mu = jnp.mean(v, axis=-1, keepdims=True)
    vc = v - mu
    var = jnp.mean(vc * vc, axis=-1, keepdims=True)
    vb = (vc * lax.rsqrt(var + LN_EPS) * ln_g_ref[...] + ln_b_ref[...]).astype(jnp.bfloat16)
    u = _gelu_tanh(proj("a_u"))

    p = proj("b_p")
    pe = jnp.concatenate([carry_p_ref[...], p], axis=0)
    carry_p_ref[...] = p[tm - POOL_HALO:, :]
    pos1 = lax.broadcasted_iota(jnp.int32, (tm, B_GROUP_DIM), 0) + (tile_in_seq * tm + 1)
    d_groups = []
    for k, win in enumerate(POOL_WINDOWS):
        cols = slice(k * B_GROUP_DIM, (k + 1) * B_GROUP_DIM)
        s = pe[:, cols]
        span = 1
        while span < win:
            s = s + pltpu.roll(s, span, axis=0)
            span *= 2
        count = jnp.minimum(pos1, win).astype(jnp.float32)
        d_groups.append((s[POOL_HALO:, :] / count - p[:, cols]).astype(jnp.bfloat16))

    gate_a = _silu(proj("a_z"))

    ri = lax.broadcasted_iota(jnp.int32, (SGU_BLOCK, SGU_BLOCK), 0)
    ci = lax.broadcasted_iota(jnp.int32, (SGU_BLOCK, SGU_BLOCK), 1)
    allowed = (ci // CHUNK) <= (ri // CHUNK)
    mixed_g = []
    for g in range(A_GROUPS):
        w_g = jnp.where(allowed, sgu_w_ref[g], 0.0).astype(jnp.bfloat16)
        cols = slice(g * A_GROUP_DIM, (g + 1) * A_GROUP_DIM)
        rhs = jnp.concatenate(
            [vb[r * SGU_BLOCK:(r + 1) * SGU_BLOCK, cols] for r in range(nblk)], axis=1)
        mixed_g.append(_dot(w_g, rhs))
    mixed = jnp.concatenate(
        [jnp.concatenate([mixed_g[g][:, r * SGU_BLOCK:(r + 1) * SGU_BLOCK]
                          for g in range(A_GROUPS)], axis=1)
         for r in range(nblk)], axis=0)
    bias_a = jnp.concatenate([sgu_bias_ref[...]] * nblk, axis=0)
    ya = (u * (mixed + bias_a) * gate_a).astype(jnp.bfloat16)

    t = proj("c_c") * proj("c_h")
    te = jnp.concatenate([carry_t_ref[...], t], axis=0)
    carry_t_ref[...] = t[tm - CONV_HALO:, :]
    t1 = pltpu.roll(te, 1, axis=0)[CONV_HALO:, :]
    t2 = pltpu.roll(te, 2, axis=0)[CONV_HALO:, :]
    conv = (conv_w_ref[0:1, :] * t2 + conv_w_ref[1:2, :] * t1 + conv_w_ref[2:3, :] * t
            + conv_b_ref[...])

    yb_lin = jnp.concatenate(
        [_wdot(d_groups[k], pool_w_ref[k]) for k in range(len(POOL_WINDOWS))], axis=1)

    sig0 = _sigmoid(proj("g0"))
    merged = sig0 * _wdot(ya, wa_ref[...])

    yb = ((yb_lin + pool_b_ref[...]) * pool_scale_ref[...] * _silu(proj("b_z"))
          ).astype(jnp.bfloat16)
    yc = (proj("c_b") * conv * _silu(proj("c_z"))).astype(jnp.bfloat16)
    merged = merged + _sigmoid(proj("g1")) * _wdot(yb, wb_ref[...])
    sig2 = _sigmoid(proj("g2"))
    merged = merged + sig2 * _wdot(yc, wc_ref[...])

    out = x + _wdot(merged.astype(jnp.bfloat16), wo_ref[...])
    if apply_final:
        ms_o = jnp.mean(out * out, axis=-1, keepdims=True)
        out = out * lax.rsqrt(ms_o + RMS_EPS) * final_g_ref[...]
    o_ref[...] = out


def _resident(shape, layer):
    nd = len(shape) - 1
    return pl.BlockSpec((None,) + tuple(shape[1:]), lambda i: (layer,) + (0,) * nd,
                        pipeline_mode=pl.Buffered(1))


def _layer_call(x2d, params, layer, *, tm, tiles_per_seq, apply_final):
    n_tok = x2d.shape[0]
    kern = functools.partial(_layer_kernel, tm=tm, tiles_per_seq=tiles_per_seq,
                             apply_final=apply_final)
    in_specs = [pl.BlockSpec((tm, D_MODEL), lambda i: (i, 0))]
    in_specs += [_resident(p.shape, layer) for p in params]
    weight_bytes = sum(p.size // p.shape[0] * p.dtype.itemsize for p in params)
    tile_bytes = tm * D_MODEL * 4
    vmem_need = weight_bytes + 4 * tile_bytes + 24 * tile_bytes
    return pl.pallas_call(
        kern,
        grid=(n_tok // tm,),
        in_specs=in_specs,
        out_specs=pl.BlockSpec((tm, D_MODEL), lambda i: (i, 0)),
        out_shape=jax.ShapeDtypeStruct((n_tok, D_MODEL), jnp.float32),
        scratch_shapes=[pltpu.VMEM((POOL_HALO, D_MODEL), jnp.float32),
                        pltpu.VMEM((CONV_HALO, D_MODEL), jnp.float32)],
        compiler_params=pltpu.CompilerParams(
            dimension_semantics=("arbitrary",),
            vmem_limit_bytes=min(vmem_need, V7X_VMEM_BYTES - 2 * 1024 * 1024)),
        name="mixer_layer_final" if apply_final else "mixer_layer",
    )(x2d, *params)


def kernel(x, norm_g, w_in, sgu_ln_g, sgu_ln_b, sgu_w, sgu_b, pool_w, pool_b, pool_scale,
           conv_w, conv_b, w_branch_a, w_branch_b, w_branch_c, w_out, final_g):
    bsz, seq, d = x.shape
    depth = w_in.shape[0]
    assert d == D_MODEL and w_in.shape[2] == N_IN
    tm = 512
    assert seq % tm == 0 and tm % SGU_BLOCK == 0
    bf = jnp.bfloat16
    rows = lambda a: a.reshape(a.shape[0], 1, -1)
    sgu_bias = jnp.repeat(jnp.swapaxes(sgu_b, 1, 2), A_GROUP_DIM, axis=2)
    final_rows = jnp.broadcast_to(final_g.reshape(1, 1, -1), (depth, 1, d))
    pk = _pack_bf16_rows
    params = (rows(norm_g), pk(w_in), rows(sgu_ln_g), rows(sgu_ln_b), sgu_w, sgu_bias,
              pk(pool_w), rows(pool_b), rows(pool_scale), conv_w, rows(conv_b),
              pk(w_branch_a), pk(w_branch_b), pk(w_branch_c), pk(w_out), final_rows)
    x2d = x.reshape(bsz * seq, d)
    for l in range(depth):
        x2d = _layer_call(x2d, params, l, tm=tm, tiles_per_seq=seq // tm,
                          apply_final=(l == depth - 1))
    return x2d.reshape(bsz, seq, d)
```

```python
import functools
import math

import jax
import jax.numpy as jnp
from jax import lax
from jax.experimental import pallas as pl
from jax.experimental.pallas import tpu as pltpu

D_MODEL = 1024
CHUNK = 64
SGU_BLOCK = 128
A_GROUPS = 8
A_GROUP_DIM = 128
POOL_WINDOWS = (2, 4, 8, 16)
B_GROUP_DIM = 256
CONV_WIDTH = 3
N_BRANCH = 3
N_IN = 12 * D_MODEL
RMS_EPS = 1e-6
LN_EPS = 1e-5

POOL_HALO = 16
CONV_HALO = 8
V7X_VMEM_BYTES = 64 * 1024 * 1024

_OFF = {name: i * D_MODEL for i, name in enumerate(
    ("a_u", "a_v", "a_z", "b_p", "b_z", "c_h", "c_b", "c_c", "c_z", "g0", "g1", "g2"))}


def _sigmoid(z):
    return 0.5 * jnp.tanh(0.5 * z) + 0.5


def _silu(z):
    return z * _sigmoid(z)


def _gelu_tanh(x):
    c = math.sqrt(2.0 / math.pi)
    hx = 0.5 * x
    return hx + hx * jnp.tanh(x * (c + (c * 0.044715) * (x * x)))


def _dot(a, b):
    return jnp.dot(a, b, preferred_element_type=jnp.float32)


def _wdot(a, w_words):
    return _dot(a, pltpu.bitcast(w_words, jnp.bfloat16))


def _pack_kernel(w_ref, o_ref):
    o_ref[...] = pltpu.bitcast(w_ref[...].astype(jnp.bfloat16), jnp.uint32)


def _pack_bf16_rows(w, rows_per_step):
    nb, k, n = w.shape
    kb = rows_per_step
    assert k % kb == 0 and kb % 16 == 0
    block_bytes = kb * n * (w.dtype.itemsize + 2)
    return pl.pallas_call(
        _pack_kernel,
        grid=(nb, k // kb),
        in_specs=[pl.BlockSpec((None, kb, n), lambda b, i: (b, i, 0))],
        out_specs=pl.BlockSpec((None, kb // 2, n), lambda b, i: (b, i, 0)),
        out_shape=jax.ShapeDtypeStruct((nb, k // 2, n), jnp.uint32),
        compiler_params=pltpu.CompilerParams(
            dimension_semantics=("arbitrary", "arbitrary"),
            vmem_limit_bytes=2 * block_bytes + 8 * 1024 * 1024),
        name="pack_weights",
    )(w)


def _layer_kernel(x_ref, norm_g_ref, w_in_ref, ln_g_ref, ln_b_ref, sgu_w_ref, sgu_bias_ref,
                  pool_w_ref, pool_b_ref, pool_scale_ref, conv_w_ref, conv_b_ref,
                  wa_ref, wb_ref, wc_ref, wo_ref, final_g_ref,
                  o_ref, carry_p_ref, carry_t_ref, *, tm, tiles_per_seq, apply_final):
    step = pl.program_id(0)
    tile_in_seq = lax.rem(step, tiles_per_seq)

    @pl.when(tile_in_seq == 0)
    def _():
        carry_p_ref[...] = jnp.zeros_like(carry_p_ref)
        carry_t_ref[...] = jnp.zeros_like(carry_t_ref)

    x = x_ref[...]
    ms = jnp.mean(x * x, axis=-1, keepdims=True)
    h = (x * lax.rsqrt(ms + RMS_EPS) * norm_g_ref[...]).astype(jnp.bfloat16)

    def proj(name):
        off = _OFF[name]
        return _wdot(h, w_in_ref[:, off:off + D_MODEL])

    nblk = tm // SGU_BLOCK

    v = _gelu_tanh(proj("a_v"))
    mu = jnp.mean(v, axis=-1, keepdims=True)
    vc = v - mu
    var = jnp.mean(vc * vc, axis=-1, keepdims=True)
    vb = (vc * lax.rsqrt(var + LN_EPS) * ln_g_ref[...] + ln_b_ref[...]).astype(jnp.bfloat16)
    sig0 = _sigmoid(proj("g0"))
    u = _gelu_tanh(proj("a_u"))
    gate_a = _silu(proj("a_z"))

    p = proj("b_p")
    pe = jnp.concatenate([carry_p_ref[...], p], axis=0)
    carry_p_ref[...] = p[tm - POOL_HALO:, :]
    pos1 = lax.broadcasted_iota(jnp.int32, (tm, B_GROUP_DIM), 0) + (tile_in_seq * tm + 1)
    d_groups = []
    for k, win in enumerate(POOL_WINDOWS):
        cols = slice(k * B_GROUP_DIM, (k + 1) * B_GROUP_DIM)
        s = pe[:, cols]
        span = 1
        while span < win:
            s = s + pltpu.roll(s, span, axis=0)
            span *= 2
        count = jnp.minimum(pos1, win).astype(jnp.float32)
        d_groups.append((s[POOL_HALO:, :] / count - p[:, cols]).astype(jnp.bfloat16))
    sig1 = _sigmoid(proj("g1"))

    ri = lax.broadcasted_iota(jnp.int32, (SGU_BLOCK, SGU_BLOCK), 0)
    ci = lax.broadcasted_iota(jnp.int32, (SGU_BLOCK, SGU_BLOCK), 1)
    allowed = (ci // CHUNK) <= (ri // CHUNK)
    mixed_g = []
    for g in range(A_GROUPS):
        w_g = jnp.where(allowed, sgu_w_ref[g], 0.0).astype(jnp.bfloat16)
        cols = slice(g * A_GROUP_DIM, (g + 1) * A_GROUP_DIM)
        rhs = jnp.concatenate(
            [vb[r * SGU_BLOCK:(r + 1) * SGU_BLOCK, cols] for r in range(nblk)], axis=1)
        mixed_g.append(_dot(w_g, rhs))
    mixed = jnp.concatenate(
        [jnp.concatenate([mixed_g[g][:, r * SGU_BLOCK:(r + 1) * SGU_BLOCK]
                          for g in range(A_GROUPS)], axis=1)
         for r in range(nblk)], axis=0)
    bias_a = jnp.concatenate([sgu_bias_ref[...]] * nblk, axis=0)
    ya = (u * (mixed + bias_a) * gate_a).astype(jnp.bfloat16)

    t = proj("c_c") * proj("c_h")
    te = jnp.concatenate([carry_t_ref[...], t], axis=0)
    carry_t_ref[...] = t[tm - CONV_HALO:, :]
    t1 = pltpu.roll(te, 1, axis=0)[CONV_HALO:, :]
    t2 = pltpu.roll(te, 2, axis=0)[CONV_HALO:, :]
    conv = (conv_w_ref[0:1, :] * t2 + conv_w_ref[1:2, :] * t1 + conv_w_ref[2:3, :] * t
            + conv_b_ref[...])
    gate_b = _silu(proj("b_z"))

    yb_lin = jnp.concatenate(
        [_wdot(d_groups[k], pool_w_ref[k]) for k in range(len(POOL_WINDOWS))], axis=1)
    yb = ((yb_lin + pool_b_ref[...]) * pool_scale_ref[...] * gate_b).astype(jnp.bfloat16)

    c_b = proj("c_b")
    merged = sig0 * _wdot(ya, wa_ref[...])
    yc = (c_b * conv * _silu(proj("c_z"))).astype(jnp.bfloat16)
    sig2 = _sigmoid(proj("g2"))
    merged = merged + sig1 * _wdot(yb, wb_ref[...])
    merged = merged + sig2 * _wdot(yc, wc_ref[...])

    out = x + _wdot(merged.astype(jnp.bfloat16), wo_ref[...])
    if apply_final:
        ms_o = jnp.mean(out * out, axis=-1, keepdims=True)
        out = out * lax.rsqrt(ms_o + RMS_EPS) * final_g_ref[...]
    o_ref[...] = out


def _resident(shape, layer):
    nd = len(shape) - 1
    return pl.BlockSpec((None,) + tuple(shape[1:]), lambda i: (layer,) + (0,) * nd,
                        pipeline_mode=pl.Buffered(1))


def _layer_call(x2d, params, layer, *, tm, tiles_per_seq, apply_final):
    n_tok = x2d.shape[0]
    kern = functools.partial(_layer_kernel, tm=tm, tiles_per_seq=tiles_per_seq,
                             apply_final=apply_final)
    in_specs = [pl.BlockSpec((tm, D_MODEL), lambda i: (i, 0))]
    in_specs += [_resident(p.shape, layer) for p in params]
    weight_bytes = sum(p.size // p.shape[0] * p.dtype.itemsize for p in params)
    tile_bytes = tm * D_MODEL * 4
    vmem_need = weight_bytes + 4 * tile_bytes + 24 * tile_bytes
    return pl.pallas_call(
        kern,
        grid=(n_tok // tm,),
        in_specs=in_specs,
        out_specs=pl.BlockSpec((tm, D_MODEL), lambda i: (i, 0)),
        out_shape=jax.ShapeDtypeStruct((n_tok, D_MODEL), jnp.float32),
        scratch_shapes=[pltpu.VMEM((POOL_HALO, D_MODEL), jnp.float32),
                        pltpu.VMEM((CONV_HALO, D_MODEL), jnp.float32)],
        compiler_params=pltpu.CompilerParams(
            dimension_semantics=("arbitrary",),
            vmem_limit_bytes=min(vmem_need, V7X_VMEM_BYTES - 2 * 1024 * 1024)),
        name="mixer_layer_final" if apply_final else "mixer_layer",
    )(x2d, *params)


def kernel(x, norm_g, w_in, sgu_ln_g, sgu_ln_b, sgu_w, sgu_b, pool_w, pool_b, pool_scale,
           conv_w, conv_b, w_branch_a, w_branch_b, w_branch_c, w_out, final_g):
    bsz, seq, d = x.shape
    depth = w_in.shape[0]
    assert d == D_MODEL and w_in.shape[2] == N_IN
    tm = 512
    assert seq % tm == 0 and tm % SGU_BLOCK == 0
    rows = lambda a: a.reshape(a.shape[0], 1, -1)
    sgu_bias = jnp.repeat(jnp.swapaxes(sgu_b, 1, 2), A_GROUP_DIM, axis=2)
    final_rows = jnp.broadcast_to(final_g.reshape(1, 1, -1), (depth, 1, d))
    pk = _pack_bf16_rows
    pool_words = pk(pool_w.reshape(depth * len(POOL_WINDOWS), B_GROUP_DIM, B_GROUP_DIM),
                    B_GROUP_DIM).reshape(depth, len(POOL_WINDOWS), B_GROUP_DIM // 2, B_GROUP_DIM)
    params = (rows(norm_g), pk(w_in, 128), rows(sgu_ln_g), rows(sgu_ln_b), sgu_w, sgu_bias,
              pool_words, rows(pool_b), rows(pool_scale), conv_w, rows(conv_b),
              pk(w_branch_a, d), pk(w_branch_b, d), pk(w_branch_c, d), pk(w_out, d),
              final_rows)
    x2d = x.reshape(bsz * seq, d)
    for l in range(depth):
        x2d = _layer_call(x2d, params, l, tm=tm, tiles_per_seq=seq // tm,
                          apply_final=(l == depth - 1))
    return x2d.reshape(bsz, seq, d)
```

```python
import functools
import math

import jax
import jax.numpy as jnp
from jax import lax
from jax.experimental import pallas as pl
from jax.experimental.pallas import tpu as pltpu

D_MODEL = 1024
CHUNK = 64
SGU_BLOCK = 128
A_GROUPS = 8
A_GROUP_DIM = 128
POOL_WINDOWS = (2, 4, 8, 16)
B_GROUP_DIM = 256
CONV_WIDTH = 3
N_BRANCH = 3
N_IN = 12 * D_MODEL
RMS_EPS = 1e-6
LN_EPS = 1e-5

POOL_HALO = 16
CONV_HALO = 8
V7X_VMEM_BYTES = 64 * 1024 * 1024

_OFF = {name: i * D_MODEL for i, name in enumerate(
    ("a_u", "a_v", "a_z", "b_p", "b_z", "c_h", "c_b", "c_c", "c_z", "g0", "g1", "g2"))}


def _sigmoid(z):
    return 0.5 * jnp.tanh(0.5 * z) + 0.5


def _silu(z):
    return z * _sigmoid(z)


def _gelu_tanh(x):
    c = math.sqrt(2.0 / math.pi)
    hx = 0.5 * x
    return hx + hx * jnp.tanh(x * (c + (c * 0.044715) * (x * x)))


def _dot(a, b):
    return jnp.dot(a, b, preferred_element_type=jnp.float32)


def _wdot(a, w_words):
    w = pltpu.bitcast(w_words, jnp.bfloat16)
    m = a.shape[0] // 2
    return jnp.concatenate([_dot(a[:m], w), _dot(a[m:], w)], axis=0)


def _pack_kernel(w_ref, o_ref):
    o_ref[...] = pltpu.bitcast(w_ref[...].astype(jnp.bfloat16), jnp.uint32)


def _pack_bf16_rows(w, rows_per_step, count=None):
    nb, k, n = w.shape
    nb = nb if count is None else count
    kb = rows_per_step
    assert k % kb == 0 and kb % 16 == 0
    block_bytes = kb * n * (w.dtype.itemsize + 2)
    return pl.pallas_call(
        _pack_kernel,
        grid=(nb, k // kb),
        in_specs=[pl.BlockSpec((None, kb, n), lambda b, i: (b, i, 0))],
        out_specs=pl.BlockSpec((None, kb // 2, n), lambda b, i: (b, i, 0)),
        out_shape=jax.ShapeDtypeStruct((nb, k // 2, n), jnp.uint32),
        compiler_params=pltpu.CompilerParams(
            dimension_semantics=("arbitrary", "arbitrary"),
            vmem_limit_bytes=2 * block_bytes + 8 * 1024 * 1024),
        name="pack_weights",
    )(w)


def _layer_kernel(x_ref, norm_g_ref, w_in_ref, ln_g_ref, ln_b_ref, sgu_w_ref, sgu_bias_ref,
                  pool_w_ref, pool_b_ref, pool_scale_ref, conv_w_ref, conv_b_ref,
                  wa_ref, wb_ref, wc_ref, wo_ref, final_g_ref, *rest,
                  tm, tiles_per_seq, apply_final):
    if apply_final:
        o_ref, carry_p_ref, carry_t_ref = rest
    else:
        w_next_ref, o_ref, w_next_words_ref, carry_p_ref, carry_t_ref = rest
        w_next_words_ref[...] = pltpu.bitcast(w_next_ref[...].astype(jnp.bfloat16), jnp.uint32)
    step = pl.program_id(0)
    tile_in_seq = lax.rem(step, tiles_per_seq)

    @pl.when(tile_in_seq == 0)
    def _():
        carry_p_ref[...] = jnp.zeros_like(carry_p_ref)
        carry_t_ref[...] = jnp.zeros_like(carry_t_ref)

    x = x_ref[...]
    ms = jnp.mean(x * x, axis=-1, keepdims=True)
    h = (x * lax.rsqrt(ms + RMS_EPS) * norm_g_ref[...]).astype(jnp.bfloat16)

    def proj(name):
        off = _OFF[name]
        return _wdot(h, w_in_ref[:, off:off + D_MODEL])

    nblk = tm // SGU_BLOCK

    v = _gelu_tanh(proj("a_v"))
    mu = jnp.mean(v, axis=-1, keepdims=True)
    vc = v - mu
    var = jnp.mean(vc * vc, axis=-1, keepdims=True)
    vb = (vc * lax.rsqrt(var + LN_EPS) * ln_g_ref[...] + ln_b_ref[...]).astype(jnp.bfloat16)
    sig0 = _sigmoid(proj("g0"))
    u = _gelu_tanh(proj("a_u"))
    gate_a = _silu(proj("a_z"))

    p = proj("b_p")
    pe = jnp.concatenate([carry_p_ref[...], p], axis=0)
    carry_p_ref[...] = p[tm - POOL_HALO:, :]
    pos1 = lax.broadcasted_iota(jnp.int32, (POOL_HALO, B_GROUP_DIM), 0) + (tile_in_seq * tm + 1)
    d_groups = []
    for k, win in enumerate(POOL_WINDOWS):
        cols = slice(k * B_GROUP_DIM, (k + 1) * B_GROUP_DIM)
        s = pe[:, cols]
        span = 1
        while span < win:
            s = s + pltpu.roll(s, span, axis=0)
            span *= 2
        s = s[POOL_HALO:, :]
        count = jnp.minimum(pos1, win).astype(jnp.float32)
        head = s[:POOL_HALO, :] / count - p[:POOL_HALO, cols]
        body = s[POOL_HALO:, :] * (1.0 / win) - p[POOL_HALO:, cols]
        d_groups.append(jnp.concatenate([head, body], axis=0).astype(jnp.bfloat16))
    sig1 = _sigmoid(proj("g1"))

    ri = lax.broadcasted_iota(jnp.int32, (SGU_BLOCK, SGU_BLOCK), 0)
    ci = lax.broadcasted_iota(jnp.int32, (SGU_BLOCK, SGU_BLOCK), 1)
    allowed = (ci // CHUNK) <= (ri // CHUNK)
    mixed_g = []
    for g in range(A_GROUPS):
        w_g = jnp.where(allowed, sgu_w_ref[g], 0.0).astype(jnp.bfloat16)
        cols = slice(g * A_GROUP_DIM, (g + 1) * A_GROUP_DIM)
        rhs = jnp.concatenate(
            [vb[r * SGU_BLOCK:(r + 1) * SGU_BLOCK, cols] for r in range(nblk)], axis=1)
        mixed_g.append(_dot(w_g, rhs))
    mixed = jnp.concatenate(
        [jnp.concatenate([mixed_g[g][:, r * SGU_BLOCK:(r + 1) * SGU_BLOCK]
                          for g in range(A_GROUPS)], axis=1)
         for r in range(nblk)], axis=0)
    bias_a = jnp.concatenate([sgu_bias_ref[...]] * nblk, axis=0)
    ya = (u * (mixed + bias_a) * gate_a).astype(jnp.bfloat16)

    t = proj("c_c") * proj("c_h")
    te = jnp.concatenate([carry_t_ref[...], t], axis=0)
    carry_t_ref[...] = t[tm - CONV_HALO:, :]
    t1 = pltpu.roll(te, 1, axis=0)[CONV_HALO:, :]
    t2 = pltpu.roll(te, 2, axis=0)[CONV_HALO:, :]
    conv = (conv_w_ref[0:1, :] * t2 + conv_w_ref[1:2, :] * t1 + conv_w_ref[2:3, :] * t
            + conv_b_ref[...])
    gate_b = _silu(proj("b_z"))

    yb_lin = jnp.concatenate(
        [_wdot(d_groups[k], pool_w_ref[k]) for k in range(len(POOL_WINDOWS))], axis=1)
    yb = ((yb_lin + pool_b_ref[...]) * pool_scale_ref[...] * gate_b).astype(jnp.bfloat16)

    c_b = proj("c_b")
    merged = sig0 * _wdot(ya, wa_ref[...])
    yc = (c_b * conv * _silu(proj("c_z"))).astype(jnp.bfloat16)
    sig2 = _sigmoid(proj("g2"))
    merged = merged + sig1 * _wdot(yb, wb_ref[...])
    merged = merged + sig2 * _wdot(yc, wc_ref[...])

    out = x + _wdot(merged.astype(jnp.bfloat16), wo_ref[...])
    if apply_final:
        ms_o = jnp.mean(out * out, axis=-1, keepdims=True)
        out = out * lax.rsqrt(ms_o + RMS_EPS) * final_g_ref[...]
    o_ref[...] = out


def _resident(shape, layer):
    nd = len(shape) - 1
    return pl.BlockSpec((None,) + tuple(shape[1:]), lambda i: (layer,) + (0,) * nd,
                        pipeline_mode=pl.Buffered(1))


def _layer_call(x2d, w_in_words, params, w_in_f32, layer, *, tm, tiles_per_seq):
    n_tok = x2d.shape[0]
    n_tiles = n_tok // tm
    depth, k_in, n_in = w_in_f32.shape
    last = layer == depth - 1
    kern = functools.partial(_layer_kernel, tm=tm, tiles_per_seq=tiles_per_seq,
                             apply_final=last)
    operands = [x2d, params[0], w_in_words] + list(params[1:])
    in_specs = [pl.BlockSpec((tm, D_MODEL), lambda i: (i, 0)),
                _resident(params[0].shape, layer),
                pl.BlockSpec((None,) + w_in_words.shape[1:], lambda i: (0, 0, 0),
                             pipeline_mode=pl.Buffered(1))]
    in_specs += [_resident(p.shape, layer) for p in params[1:]]
    out_shape = [jax.ShapeDtypeStruct((n_tok, D_MODEL), jnp.float32)]
    out_specs = [pl.BlockSpec((tm, D_MODEL), lambda i: (i, 0))]
    weight_bytes = (sum(p.size // p.shape[0] * p.dtype.itemsize for p in params)
                    + w_in_words.size * 4)
    tile_bytes = tm * D_MODEL * 4
    slice_bytes = 0
    if not last:
        assert k_in % (16 * n_tiles) == 0
        kb = k_in // n_tiles
        operands.append(w_in_f32)
        in_specs.append(pl.BlockSpec((None, kb, n_in), lambda i: (layer + 1, i, 0)))
        out_shape.append(jax.ShapeDtypeStruct((1, k_in // 2, n_in), jnp.uint32))
        out_specs.append(pl.BlockSpec((None, kb // 2, n_in), lambda i: (0, i, 0)))
        slice_bytes = 2 * kb * n_in * 6
    vmem_need = weight_bytes + 4 * tile_bytes + 24 * tile_bytes + slice_bytes
    res = pl.pallas_call(
        kern,
        grid=(n_tiles,),
        in_specs=in_specs,
        out_specs=out_specs,
        out_shape=out_shape,
        scratch_shapes=[pltpu.VMEM((POOL_HALO, D_MODEL), jnp.float32),
                        pltpu.VMEM((CONV_HALO, D_MODEL), jnp.float32)],
        compiler_params=pltpu.CompilerParams(
            dimension_semantics=("arbitrary",),
            vmem_limit_bytes=min(vmem_need, V7X_VMEM_BYTES - 2 * 1024 * 1024)),
        name="mixer_layer_final" if last else "mixer_layer",
    )(*operands)
    return (res[0], None) if last else (res[0], res[1])


def kernel(x, norm_g, w_in, sgu_ln_g, sgu_ln_b, sgu_w, sgu_b, pool_w, pool_b, pool_scale,
           conv_w, conv_b, w_branch_a, w_branch_b, w_branch_c, w_out, final_g):
    bsz, seq, d = x.shape
    depth = w_in.shape[0]
    assert d == D_MODEL and w_in.shape[2] == N_IN
    tm = 512
    assert seq % tm == 0 and tm % SGU_BLOCK == 0
    rows = lambda a: a.reshape(a.shape[0], 1, -1)
    sgu_bias = jnp.repeat(jnp.swapaxes(sgu_b, 1, 2), A_GROUP_DIM, axis=2)
    final_rows = jnp.broadcast_to(final_g.reshape(1, 1, -1), (depth, 1, d))
    pk = _pack_bf16_rows
    pool_words = pk(pool_w.reshape(depth * len(POOL_WINDOWS), B_GROUP_DIM, B_GROUP_DIM),
                    B_GROUP_DIM).reshape(depth, len(POOL_WINDOWS), B_GROUP_DIM // 2, B_GROUP_DIM)
    params = (rows(norm_g), rows(sgu_ln_g), rows(sgu_ln_b), sgu_w, sgu_bias,
              pool_words, rows(pool_b), rows(pool_scale), conv_w, rows(conv_b),
              pk(w_branch_a, d), pk(w_branch_b, d), pk(w_branch_c, d), pk(w_out, d),
              final_rows)
    w_in_words = pk(w_in, 128, count=1)
    x2d = x.reshape(bsz * seq, d)
    for l in range(depth):
        x2d, w_in_words = _layer_call(x2d, w_in_words, params, w_in, l, tm=tm,
                                      tiles_per_seq=seq // tm)
    return x2d.reshape(bsz, seq, d)
```

```python
import functools
import math

import jax
import jax.numpy as jnp
from jax import lax
from jax.experimental import pallas as pl
from jax.experimental.pallas import tpu as pltpu

D_MODEL = 1024
CHUNK = 64
SGU_BLOCK = 128
A_GROUPS = 8
A_GROUP_DIM = 128
POOL_WINDOWS = (2, 4, 8, 16)
B_GROUP_DIM = 256
CONV_WIDTH = 3
N_BRANCH = 3
N_IN = 12 * D_MODEL
RMS_EPS = 1e-6
LN_EPS = 1e-5

POOL_HALO = 16
CONV_HALO = 8
V7X_VMEM_BYTES = 64 * 1024 * 1024

_OFF = {name: i * D_MODEL for i, name in enumerate(
    ("a_u", "a_v", "a_z", "b_p", "b_z", "c_h", "c_b", "c_c", "c_z", "g0", "g1", "g2"))}


def _sigmoid(z):
    return 0.5 * jnp.tanh(0.5 * z) + 0.5


def _silu(z):
    return z * _sigmoid(z)


def _gelu_tanh(x):
    c = math.sqrt(2.0 / math.pi)
    hx = 0.5 * x
    return hx + hx * jnp.tanh(x * (c + (c * 0.044715) * (x * x)))


def _dot(a, b):
    return jnp.dot(a, b, preferred_element_type=jnp.float32)


def _wdot(a, w_words):
    w = pltpu.bitcast(w_words, jnp.bfloat16)
    m = a.shape[0] // 2
    return jnp.concatenate([_dot(a[:m], w), _dot(a[m:], w)], axis=0)


def _pack_kernel(w_ref, o_ref):
    o_ref[...] = pltpu.bitcast(w_ref[...].astype(jnp.bfloat16), jnp.uint32)


def _pack_bf16_rows(w, rows_per_step, count=None):
    nb, k, n = w.shape
    nb = nb if count is None else count
    kb = rows_per_step
    assert k % kb == 0 and kb % 16 == 0
    block_bytes = kb * n * (w.dtype.itemsize + 2)
    return pl.pallas_call(
        _pack_kernel,
        grid=(nb, k // kb),
        in_specs=[pl.BlockSpec((None, kb, n), lambda b, i: (b, i, 0))],
        out_specs=pl.BlockSpec((None, kb // 2, n), lambda b, i: (b, i, 0)),
        out_shape=jax.ShapeDtypeStruct((nb, k // 2, n), jnp.uint32),
        compiler_params=pltpu.CompilerParams(
            dimension_semantics=("arbitrary", "arbitrary"),
            vmem_limit_bytes=2 * block_bytes + 8 * 1024 * 1024),
        name="pack_weights",
    )(w)


def _layer_kernel(x_ref, norm_g_ref, w_in_ref, ln_g_ref, ln_b_ref, sgu_w_ref, sgu_bias_ref,
                  pool_w_ref, pool_b_ref, pool_scale_ref, conv_w_ref, conv_b_ref,
                  wa_ref, wb_ref, wc_ref, wo_ref, final_g_ref, *rest,
                  tm, tiles_per_seq, layer, apply_final):
    if apply_final:
        o_ref, carry_p_ref, carry_t_ref = rest
    else:
        w_next_ref, o_ref, w_next_words_ref, carry_p_ref, carry_t_ref = rest
        w_next_words_ref[...] = pltpu.bitcast(w_next_ref[...].astype(jnp.bfloat16), jnp.uint32)
    step = pl.program_id(0)
    tile_in_seq = lax.rem(step, tiles_per_seq)

    def vec(ref):
        return ref[layer:layer + 1, :]

    @pl.when(tile_in_seq == 0)
    def _():
        carry_p_ref[...] = jnp.zeros_like(carry_p_ref)
        carry_t_ref[...] = jnp.zeros_like(carry_t_ref)

    x = x_ref[...]
    ms = jnp.mean(x * x, axis=-1, keepdims=True)
    h = (x * lax.rsqrt(ms + RMS_EPS) * vec(norm_g_ref)).astype(jnp.bfloat16)

    def proj(name):
        off = _OFF[name]
        return _wdot(h, w_in_ref[:, off:off + D_MODEL])

    nblk = tm // SGU_BLOCK

    v = _gelu_tanh(proj("a_v"))
    mu = jnp.mean(v, axis=-1, keepdims=True)
    vc = v - mu
    var = jnp.mean(vc * vc, axis=-1, keepdims=True)
    vb = (vc * lax.rsqrt(var + LN_EPS) * vec(ln_g_ref) + vec(ln_b_ref)).astype(jnp.bfloat16)
    sig0 = _sigmoid(proj("g0"))
    u = _gelu_tanh(proj("a_u"))
    gate_a = _silu(proj("a_z"))

    p = proj("b_p")
    pe = jnp.concatenate([carry_p_ref[...], p], axis=0)
    carry_p_ref[...] = p[tm - POOL_HALO:, :]
    pos1 = lax.broadcasted_iota(jnp.int32, (POOL_HALO, B_GROUP_DIM), 0) + (tile_in_seq * tm + 1)
    d_groups = []
    for k, win in enumerate(POOL_WINDOWS):
        cols = slice(k * B_GROUP_DIM, (k + 1) * B_GROUP_DIM)
        s = pe[:, cols]
        span = 1
        while span < win:
            s = s + pltpu.roll(s, span, axis=0)
            span *= 2
        s = s[POOL_HALO:, :]
        count = jnp.minimum(pos1, win).astype(jnp.float32)
        head = s[:POOL_HALO, :] / count - p[:POOL_HALO, cols]
        body = s[POOL_HALO:, :] * (1.0 / win) - p[POOL_HALO:, cols]
        d_groups.append(jnp.concatenate([head, body], axis=0).astype(jnp.bfloat16))
    sig1 = _sigmoid(proj("g1"))

    ri = lax.broadcasted_iota(jnp.int32, (SGU_BLOCK, SGU_BLOCK), 0)
    ci = lax.broadcasted_iota(jnp.int32, (SGU_BLOCK, SGU_BLOCK), 1)
    allowed = (ci // CHUNK) <= (ri // CHUNK)
    mixed_g = []
    for g in range(A_GROUPS):
        w_g = jnp.where(allowed, sgu_w_ref[g], 0.0).astype(jnp.bfloat16)
        cols = slice(g * A_GROUP_DIM, (g + 1) * A_GROUP_DIM)
        rhs = jnp.concatenate(
            [vb[r * SGU_BLOCK:(r + 1) * SGU_BLOCK, cols] for r in range(nblk)], axis=1)
        mixed_g.append(_dot(w_g, rhs))
    mixed = jnp.concatenate(
        [jnp.concatenate([mixed_g[g][:, r * SGU_BLOCK:(r + 1) * SGU_BLOCK]
                          for g in range(A_GROUPS)], axis=1)
         for r in range(nblk)], axis=0)
    bias_a = jnp.concatenate([sgu_bias_ref[...]] * nblk, axis=0)
    ya = (u * (mixed + bias_a) * gate_a).astype(jnp.bfloat16)

    t = proj("c_c") * proj("c_h")
    te = jnp.concatenate([carry_t_ref[...], t], axis=0)
    carry_t_ref[...] = t[tm - CONV_HALO:, :]
    t1 = pltpu.roll(te, 1, axis=0)[CONV_HALO:, :]
    t2 = pltpu.roll(te, 2, axis=0)[CONV_HALO:, :]
    conv = (conv_w_ref[0:1, :] * t2 + conv_w_ref[1:2, :] * t1 + conv_w_ref[2:3, :] * t
            + vec(conv_b_ref))
    gate_b = _silu(proj("b_z"))

    yb_lin = jnp.concatenate(
        [_wdot(d_groups[k], pool_w_ref[k]) for k in range(len(POOL_WINDOWS))], axis=1)
    yb = ((yb_lin + vec(pool_b_ref)) * vec(pool_scale_ref) * gate_b).astype(jnp.bfloat16)

    c_b = proj("c_b")
    merged = sig0 * _wdot(ya, wa_ref[...])
    yc = (c_b * conv * _silu(proj("c_z"))).astype(jnp.bfloat16)
    sig2 = _sigmoid(proj("g2"))
    merged = merged + sig1 * _wdot(yb, wb_ref[...])
    merged = merged + sig2 * _wdot(yc, wc_ref[...])

    out = x + _wdot(merged.astype(jnp.bfloat16), wo_ref[...])
    if apply_final:
        ms_o = jnp.mean(out * out, axis=-1, keepdims=True)
        out = out * lax.rsqrt(ms_o + RMS_EPS) * final_g_ref[...]
    o_ref[...] = out


def _resident(arr, layer):
    if layer is None:
        nd = arr.ndim
        return pl.BlockSpec(arr.shape, lambda i: (0,) * nd, pipeline_mode=pl.Buffered(1))
    nd = arr.ndim - 1
    return pl.BlockSpec((None,) + tuple(arr.shape[1:]), lambda i: (layer,) + (0,) * nd,
                        pipeline_mode=pl.Buffered(1))


_TEMP_TILES = 8
_PACK_ROWS_W_IN = 128


def _layer_call(x2d, w_in_words, tables, weights, w_in_f32, layer, *, tm, tiles_per_seq):
    n_tok = x2d.shape[0]
    n_tiles = n_tok // tm
    depth, k_in, n_in = w_in_f32.shape
    last = layer == depth - 1
    kern = functools.partial(_layer_kernel, tm=tm, tiles_per_seq=tiles_per_seq, layer=layer,
                             apply_final=last)
    (norm_g, ln_g, ln_b, pool_b, pool_scale, conv_b, final_g) = tables
    (sgu_w, sgu_bias, pool_words, conv_w, wa, wb, wc, wo) = weights
    whole = lambda a: (a, _resident(a, None))
    mine = lambda a: (a, _resident(a, layer))
    bound = [(x2d, pl.BlockSpec((tm, D_MODEL), lambda i: (i, 0))),
             whole(norm_g), (w_in_words, _resident(w_in_words, 0)), whole(ln_g), whole(ln_b),
             mine(sgu_w), mine(sgu_bias), mine(pool_words), whole(pool_b), whole(pool_scale),
             mine(conv_w), whole(conv_b), mine(wa), mine(wb), mine(wc), mine(wo),
             whole(final_g)]
    out_shape = [jax.ShapeDtypeStruct((n_tok, D_MODEL), jnp.float32)]
    out_specs = [pl.BlockSpec((tm, D_MODEL), lambda i: (i, 0))]
    resident_bytes = (sum(a.size * a.dtype.itemsize for a in tables) + w_in_words.size * 4
                      + sum(a.size // a.shape[0] * a.dtype.itemsize for a in weights))
    tile_bytes = tm * D_MODEL * 4
    slice_bytes = 0
    if not last:
        assert k_in % (16 * n_tiles) == 0
        kb = k_in // n_tiles
        bound.append((w_in_f32, pl.BlockSpec((None, kb, n_in), lambda i: (layer + 1, i, 0))))
        out_shape.append(jax.ShapeDtypeStruct((1, k_in // 2, n_in), jnp.uint32))
        out_specs.append(pl.BlockSpec((None, kb // 2, n_in), lambda i: (0, i, 0)))
        slice_bytes = 2 * kb * n_in * (4 + 2)
    vmem_need = resident_bytes + 4 * tile_bytes + _TEMP_TILES * tile_bytes + slice_bytes
    assert vmem_need <= V7X_VMEM_BYTES
    res = pl.pallas_call(
        kern,
        grid=(n_tiles,),
        in_specs=[spec for _, spec in bound],
        out_specs=out_specs,
        out_shape=out_shape,
        scratch_shapes=[pltpu.VMEM((POOL_HALO, D_MODEL), jnp.float32),
                        pltpu.VMEM((CONV_HALO, D_MODEL), jnp.float32)],
        compiler_params=pltpu.CompilerParams(
            dimension_semantics=("arbitrary",), vmem_limit_bytes=vmem_need),
        name="mixer_layer_final" if last else "mixer_layer",
    )(*[a for a, _ in bound])
    return (res[0], None) if last else (res[0], res[1])


def kernel(x, norm_g, w_in, sgu_ln_g, sgu_ln_b, sgu_w, sgu_b, pool_w, pool_b, pool_scale,
           conv_w, conv_b, w_branch_a, w_branch_b, w_branch_c, w_out, final_g):
    bsz, seq, d = x.shape
    depth = w_in.shape[0]
    assert d == D_MODEL and w_in.shape[2] == N_IN
    tm = 512
    assert seq % tm == 0 and tm % SGU_BLOCK == 0
    sgu_bias = jnp.repeat(jnp.swapaxes(sgu_b, 1, 2), A_GROUP_DIM, axis=2)
    pk = _pack_bf16_rows
    pool_words = pk(pool_w.reshape(depth * len(POOL_WINDOWS), B_GROUP_DIM, B_GROUP_DIM),
                    B_GROUP_DIM).reshape(depth, len(POOL_WINDOWS), B_GROUP_DIM // 2, B_GROUP_DIM)
    tables = (norm_g, sgu_ln_g, sgu_ln_b, pool_b, pool_scale, conv_b, final_g.reshape(1, d))
    weights = (sgu_w, sgu_bias, pool_words, conv_w,
               pk(w_branch_a, d), pk(w_branch_b, d), pk(w_branch_c, d), pk(w_out, d))
    w_in_words = pk(w_in, _PACK_ROWS_W_IN, count=1)
    x2d = x.reshape(bsz * seq, d)
    for l in range(depth):
        x2d, w_in_words = _layer_call(x2d, w_in_words, tables, weights, w_in, l, tm=tm,
                                      tiles_per_seq=seq // tm)
    return x2d.reshape(bsz, seq, d)
```
